```python
import jax, jax.numpy as jnp
from jax import lax
import numpy as np

D_MODEL = 2048
BATCH = 2
SEQ = 16384
DEPTH = 2
DEC_BATCH = 32
DEC_SEQ = 64
PAST_LEN = 2048

CHUNK = 64
W_A = D_MODEL
N_LRU_HEADS = 16
LRU_HEAD_DIM = W_A // N_LRU_HEADS
CONV_WIDTH = 4
LRU_C = 8.0
W_B = D_MODEL
POOL_WINDOWS = (2, 4, 8, 16)
N_POOL_GROUPS = 4
POOL_GROUP_DIM = W_B // N_POOL_GROUPS
POOL_PAD = POOL_WINDOWS[-1] - 1
D_IN = 2 * W_A + 2 * W_B + 2 * D_MODEL
EPS = 1e-6

kernel_name = "hybrid_rglru_pool_stream_step"


def rms_norm(x, g):
    xf = x.astype(jnp.float32)
    y = xf * lax.rsqrt(jnp.mean(xf * xf, axis=-1, keepdims=True) + EPS)
    return (y * g.astype(jnp.float32)).astype(x.dtype)


def causal_dwconv(xa, buf, w, b):
    T = xa.shape[1]
    xp = jnp.concatenate([buf.astype(xa.dtype), xa], axis=1)
    y = b
    for k in range(CONV_WIDTH):
        y = y + xp[:, k:k + T] * w[k]
    return y, xp[:, -(CONV_WIDTH - 1):]


def block_diag(x, w, b):
    bsz, T, _ = x.shape
    xh = x.reshape(bsz, T, N_LRU_HEADS, LRU_HEAD_DIM)
    return jnp.einsum('bthd,hde->bthe', xh, w).reshape(bsz, T, W_A) + b


def rg_lru(x, h0, wa, ba, wx, bx, lam):
    r = jax.nn.sigmoid(block_diag(x, wa, ba).astype(jnp.float32))
    i = jax.nn.sigmoid(block_diag(x, wx, bx).astype(jnp.float32))
    log_a = -LRU_C * r * jax.nn.softplus(-lam.astype(jnp.float32))
    a = jnp.exp(log_a)
    mult = jnp.sqrt(jnp.maximum(-jnp.expm1(2.0 * log_a), 0.0))
    u = mult * i * x.astype(jnp.float32)

    def combine(left, right):
        a1, b1 = left
        a2, b2 = right
        return a1 * a2, a2 * b1 + b2

    a_cum, b_cum = lax.associative_scan(combine, (a, u), axis=1)
    h = a_cum * h0.astype(jnp.float32)[:, None, :] + b_cum
    return h, h[:, -1]


def pool_mix(xb, buf, p0, pool_w, pool_scale):
    bsz, T, _ = xb.shape
    xp = jnp.concatenate([buf.astype(xb.dtype), xb], axis=1).astype(jnp.float32)
    S = jnp.concatenate([jnp.zeros((bsz, 1, W_B), jnp.float32), jnp.cumsum(xp, axis=1)], axis=1)
    pos = p0 + jnp.arange(T)
    outs = []
    for g, w in enumerate(POOL_WINDOWS):
        sl = slice(g * POOL_GROUP_DIM, (g + 1) * POOL_GROUP_DIM)
        win = S[:, POOL_PAD + 1:POOL_PAD + 1 + T, sl] - S[:, POOL_PAD + 1 - w:POOL_PAD + 1 - w + T, sl]
        cnt = jnp.minimum(w, pos + 1).astype(jnp.float32)[None, :, None]
        outs.append(win / cnt - xp[:, POOL_PAD:, sl])
    d = jnp.stack(outs, axis=2).astype(xb.dtype)
    y = jnp.einsum('btgc,gce->btge', d, pool_w).reshape(bsz, T, W_B)
    return y * pool_scale


def layer(x, c, conv_buf, h0, pool_buf, p0, norm_g, w_ada, b_ada, w_in, conv_w, conv_b,
          lru_wa, lru_ba, lru_wx, lru_bx, lru_lam, pool_w, pool_scale, w_proj_a, w_proj_b, w_out):
    mod = jax.nn.silu(c) @ w_ada + b_ada
    shift, scale, gate = jnp.split(mod, 3, axis=-1)
    h = rms_norm(x, norm_g) * (1.0 + scale[:, None]) + shift[:, None]
    z = h @ w_in
    xa, ga, xb, gb, ma, mb = jnp.split(
        z, [W_A, 2 * W_A, 2 * W_A + W_B, 2 * W_A + 2 * W_B, 2 * W_A + 2 * W_B + D_MODEL], axis=-1)
    xa_c, conv_new = causal_dwconv(xa, conv_buf, conv_w, conv_b)
    hs, h_last = rg_lru(xa_c, h0, lru_wa, lru_ba, lru_wx, lru_bx, lru_lam)
    ya = hs.astype(x.dtype) * jax.nn.silu(ga)
    yb = pool_mix(xb, pool_buf, p0, pool_w, pool_scale) * jax.nn.silu(gb)
    pool_new = jnp.concatenate([pool_buf.astype(xb.dtype), xb], axis=1)[:, -POOL_PAD:]
    m = jax.nn.sigmoid(ma) * (ya @ w_proj_a) + jax.nn.sigmoid(mb) * (yb @ w_proj_b)
    x = x + gate[:, None] * (m @ w_out)
    return x, conv_new, h_last.astype(x.dtype), pool_new


def setup_inputs(seed: int = 0) -> dict:
    key = jax.random.key(seed)
    ks = jax.random.split(key, 32)
    f32 = jnp.float32
    nrm = lambda k, s, sc: jax.random.normal(k, s, f32) * sc
    a0 = jax.random.uniform(ks[12], (DEPTH, W_A), f32, 0.9, 0.999)
    sig = a0 ** (1.0 / LRU_C)
    lru_lam = jnp.log(sig / (1.0 - sig))
    return {
        "x_prompt": nrm(ks[0], (BATCH, SEQ, D_MODEL), 1.0),
        "x_sample": nrm(ks[1], (DEC_BATCH, DEC_SEQ, D_MODEL), 1.0),
        "c_prompt": nrm(ks[2], (BATCH, D_MODEL), 1.0),
        "c_sample": nrm(ks[3], (DEC_BATCH, D_MODEL), 1.0),
        "state_conv": nrm(ks[4], (DEPTH, DEC_BATCH, CONV_WIDTH - 1, W_A), 1.0),
        "state_lru": nrm(ks[5], (DEPTH, DEC_BATCH, W_A), 0.5),
        "state_pool": nrm(ks[6], (DEPTH, DEC_BATCH, POOL_PAD, W_B), 1.0),
        "norm_g": 1.0 + nrm(ks[7], (DEPTH, D_MODEL), 0.05),
        "w_ada": nrm(ks[8], (DEPTH, D_MODEL, 3 * D_MODEL), D_MODEL ** -0.5),
        "b_ada": nrm(ks[9], (DEPTH, 3 * D_MODEL), 0.01),
        "w_in": nrm(ks[10], (DEPTH, D_MODEL, D_IN), D_MODEL ** -0.5),
        "conv_w": nrm(ks[11], (DEPTH, CONV_WIDTH, W_A), CONV_WIDTH ** -0.5),
        "conv_b": nrm(ks[13], (DEPTH, W_A), 0.01),
        "lru_wa": nrm(ks[14], (DEPTH, N_LRU_HEADS, LRU_HEAD_DIM, LRU_HEAD_DIM), LRU_HEAD_DIM ** -0.5),
        "lru_ba": nrm(ks[15], (DEPTH, W_A), 0.01),
        "lru_wx": nrm(ks[16], (DEPTH, N_LRU_HEADS, LRU_HEAD_DIM, LRU_HEAD_DIM), LRU_HEAD_DIM ** -0.5),
        "lru_bx": nrm(ks[17], (DEPTH, W_A), 0.01),
        "lru_lam": lru_lam,
        "pool_w": nrm(ks[18], (DEPTH, N_POOL_GROUPS, POOL_GROUP_DIM, POOL_GROUP_DIM), POOL_GROUP_DIM ** -0.5),
        "pool_scale": 1.0 + nrm(ks[19], (DEPTH, W_B), 0.1),
        "w_proj_a": nrm(ks[20], (DEPTH, W_A, D_MODEL), W_A ** -0.5),
        "w_proj_b": nrm(ks[21], (DEPTH, W_B, D_MODEL), W_B ** -0.5),
        "w_out": nrm(ks[22], (DEPTH, D_MODEL, D_MODEL), D_MODEL ** -0.5),
        "final_g": 1.0 + nrm(ks[23], (D_MODEL,), 0.05),
    }


def reference(x_prompt, x_sample, c_prompt, c_sample, state_conv, state_lru, state_pool,
              norm_g, w_ada, b_ada, w_in, conv_w, conv_b, lru_wa, lru_ba, lru_wx, lru_bx,
              lru_lam, pool_w, pool_scale, w_proj_a, w_proj_b, w_out, final_g):
    xp, xs = x_prompt, x_sample
    bp = x_prompt.shape[0]
    conv_p, lru_p, pool_p = [], [], []
    conv_s, lru_s, pool_s = [], [], []
    for l in range(DEPTH):
        wl = (norm_g[l], w_ada[l], b_ada[l], w_in[l], conv_w[l], conv_b[l], lru_wa[l], lru_ba[l],
              lru_wx[l], lru_bx[l], lru_lam[l], pool_w[l], pool_scale[l], w_proj_a[l], w_proj_b[l], w_out[l])
        xp, cb, hl, pb = layer(xp, c_prompt,
                               jnp.zeros((bp, CONV_WIDTH - 1, W_A), xp.dtype),
                               jnp.zeros((bp, W_A), jnp.float32),
                               jnp.zeros((bp, POOL_PAD, W_B), xp.dtype),
                               0, *wl)
        conv_p.append(cb); lru_p.append(hl); pool_p.append(pb)
        xs, cb, hl, pb = layer(xs, c_sample, state_conv[l], state_lru[l], state_pool[l],
                               PAST_LEN, *wl)
        conv_s.append(cb); lru_s.append(hl); pool_s.append(pb)
    y_prompt = rms_norm(xp, final_g)
    y_sample = rms_norm(xs, final_g)
    return (y_prompt, y_sample,
            jnp.stack(conv_p), jnp.stack(lru_p), jnp.stack(pool_p),
            jnp.stack(conv_s), jnp.stack(lru_s), jnp.stack(pool_s))
```

```python
import functools

import jax
import jax.numpy as jnp
from jax import lax
from jax.experimental import pallas as pl
from jax.experimental.pallas import tpu as pltpu

D_MODEL = 2048
DEPTH = 2
W_A = D_MODEL
N_LRU_HEADS = 16
LRU_HEAD_DIM = W_A // N_LRU_HEADS
CONV_WIDTH = 4
LRU_C = 8.0
W_B = D_MODEL
POOL_WINDOWS = (2, 4, 8, 16)
N_POOL_GROUPS = 4
POOL_GROUP_DIM = W_B // N_POOL_GROUPS
POOL_PAD = POOL_WINDOWS[-1] - 1
D_IN = 2 * W_A + 2 * W_B + 2 * D_MODEL
EPS = 1e-6
PAST_LEN = 2048

F32 = jnp.float32
BF16 = jnp.bfloat16

V7X_VMEM_LIMIT_BYTES = 56 * 1024 * 1024
SUBLANES = 8
POOL_HIST = POOL_PAD + 1


def _params(semantics):
    return pltpu.CompilerParams(dimension_semantics=semantics,
                                vmem_limit_bytes=V7X_VMEM_LIMIT_BYTES)


def _sigmoid(x):
    return jax.nn.sigmoid(x)


def _mod_kernel(c_ref, w_ref, b_ref, o_ref):
    c = c_ref[...]
    s = (c * _sigmoid(c)).astype(BF16)
    o_ref[...] = jnp.dot(s, w_ref[...].astype(BF16), preferred_element_type=F32) + b_ref[...]


def _modulation(c_all, w_ada, b_ada, tn=512):
    nb = c_all.shape[0]
    return pl.pallas_call(
        _mod_kernel,
        grid=(DEPTH, 3 * D_MODEL // tn),
        in_specs=[
            pl.BlockSpec((nb, D_MODEL), lambda l, j: (0, 0)),
            pl.BlockSpec((None, D_MODEL, tn), lambda l, j: (l, 0, j)),
            pl.BlockSpec((None, 1, tn), lambda l, j: (l, 0, j)),
        ],
        out_specs=pl.BlockSpec((None, nb, tn), lambda l, j: (l, 0, j)),
        out_shape=jax.ShapeDtypeStruct((DEPTH, nb, 3 * D_MODEL), F32),
        compiler_params=_params(("parallel", "parallel")),
        name="adaln_mod",
    )(c_all, w_ada, b_ada.reshape(DEPTH, 1, 3 * D_MODEL))


def _inproj_kernel(x_ref, mod_ref, g_ref, w_ref, z_ref, h_scr):
    bb, tt, _ = x_ref.shape

    @pl.when(pl.program_id(2) == 0)
    def _():
        x = x_ref[...]
        y = x * lax.rsqrt(jnp.mean(x * x, axis=-1, keepdims=True) + EPS) * g_ref[...]
        h = y * (1.0 + mod_ref[:, 1:2, :]) + mod_ref[:, 0:1, :]
        h_scr[...] = h.reshape(bb * tt, D_MODEL).astype(BF16)

    z = jnp.dot(h_scr[...], w_ref[...], preferred_element_type=F32)
    z_ref[...] = z.reshape(bb, tt, z.shape[-1])


def _inproj(x, mod, norm_g, w_in, l, bb, tt, tn=1024):
    B, T, _ = x.shape
    return pl.pallas_call(
        _inproj_kernel,
        grid=(B // bb, T // tt, D_IN // tn),
        in_specs=[
            pl.BlockSpec((bb, tt, D_MODEL), lambda b, t, j: (b, t, 0)),
            pl.BlockSpec((bb, 3, D_MODEL), lambda b, t, j: (b, 0, 0)),
            pl.BlockSpec((None, 1, D_MODEL), lambda b, t, j: (l, 0, 0)),
            pl.BlockSpec((None, D_MODEL, tn), lambda b, t, j: (l, 0, j)),
        ],
        out_specs=pl.BlockSpec((bb, tt, tn), lambda b, t, j: (b, t, j)),
        out_shape=jax.ShapeDtypeStruct((B, T, D_IN), F32),
        scratch_shapes=[pltpu.VMEM((bb * tt, D_MODEL), BF16)],
        compiler_params=_params(("parallel", "parallel", "arbitrary")),
        name="inproj",
    )(x, mod, norm_g.reshape(DEPTH, 1, D_MODEL), w_in)


def _linear_scan(a, u):
    tt = a.shape[0]
    row = lax.broadcasted_iota(jnp.int32, a.shape, 0)
    d = 1
    while d < tt:
        valid = row >= d
        a_sh = pltpu.roll(a, d, axis=0)
        u_sh = pltpu.roll(u, d, axis=0)
        u = jnp.where(valid, a * u_sh + u, u)
        a = jnp.where(valid, a * a_sh, a)
        d *= 2
    return a, u


def _branch_a_kernel(xa_ref, ga_ref, cbuf_ref, h0_ref, cw_ref, cb_ref, wa_ref, ba_ref, wx_ref, bx_ref,
                     lam_ref, ya_ref, cnew_ref, hlast_ref, xp_scr, h_scr):
    t = pl.program_id(2)
    nt = pl.num_programs(2)
    tt, tc = xa_ref.shape
    npad = SUBLANES

    @pl.when(t == 0)
    def _():
        xp_scr[npad - (CONV_WIDTH - 1):npad, :] = cbuf_ref[...]
        h_scr[...] = h0_ref[...]

    xa = xa_ref[...]
    xp_scr[npad:npad + tt, :] = xa
    cw = cw_ref[...]
    xc = cb_ref[...] + xp_scr[npad - 3:npad - 3 + tt, :] * cw[0:1, :]
    xc = xc + xp_scr[npad - 2:npad - 2 + tt, :] * cw[1:2, :]
    xc = xc + xp_scr[npad - 1:npad - 1 + tt, :] * cw[2:3, :]
    xc = xc + xa * cw[3:4, :]
    tail = xp_scr[npad + tt - (CONV_WIDTH - 1):npad + tt, :]
    xp_scr[npad - (CONV_WIDTH - 1):npad, :] = tail

    xh = xc.astype(BF16)
    r_parts, i_parts = [], []
    for hh in range(tc // LRU_HEAD_DIM):
        xs = xh[:, hh * LRU_HEAD_DIM:(hh + 1) * LRU_HEAD_DIM]
        r_parts.append(jnp.dot(xs, wa_ref[hh], preferred_element_type=F32))
        i_parts.append(jnp.dot(xs, wx_ref[hh], preferred_element_type=F32))
    r = _sigmoid(jnp.concatenate(r_parts, axis=-1) + ba_ref[...])
    i = _sigmoid(jnp.concatenate(i_parts, axis=-1) + bx_ref[...])
    log_a = (-LRU_C) * r * jax.nn.softplus(-lam_ref[...])
    a = jnp.exp(log_a)
    th = jnp.tanh(log_a)
    neg_expm1 = (-2.0 * th) / (1.0 - th)
    mult = jnp.sqrt(jnp.maximum(neg_expm1, 0.0))
    u = mult * i * xc

    a_cum, b_cum = _linear_scan(a, u)
    hs = a_cum * h_scr[...] + b_cum
    h_last = hs[tt - 1:tt, :]
    h_scr[...] = h_last

    ga = ga_ref[...]
    ya_ref[...] = (hs * (ga * _sigmoid(ga))).astype(BF16)

    @pl.when(t == nt - 1)
    def _():
        cnew_ref[...] = tail
        hlast_ref[...] = h_last


def _branch_a(z, conv_buf, h0, conv_w, conv_b, lru_wa, lru_ba, lru_wx, lru_bx, lru_lam, l, tt, tc=512):
    B, T, _ = z.shape
    nc = W_A // tc
    vec = lambda: pl.BlockSpec((None, 1, tc), lambda b, c, t: (l, 0, c))
    heads = lambda: pl.BlockSpec((None, tc // LRU_HEAD_DIM, LRU_HEAD_DIM, LRU_HEAD_DIM),
                                 lambda b, c, t: (l, c, 0, 0))
    return pl.pallas_call(
        _branch_a_kernel,
        grid=(B, nc, T // tt),
        in_specs=[
            pl.BlockSpec((None, tt, tc), lambda b, c, t: (b, t, c)),
            pl.BlockSpec((None, tt, tc), lambda b, c, t: (b, t, nc + c)),
            pl.BlockSpec((None, CONV_WIDTH - 1, tc), lambda b, c, t: (b, 0, c)),
            pl.BlockSpec((None, 1, tc), lambda b, c, t: (b, 0, c)),
            pl.BlockSpec((None, CONV_WIDTH, tc), lambda b, c, t: (l, 0, c)),
            vec(), heads(), vec(), heads(), vec(), vec(),
        ],
        out_specs=[
            pl.BlockSpec((None, tt, tc), lambda b, c, t: (b, t, c)),
            pl.BlockSpec((None, CONV_WIDTH - 1, tc), lambda b, c, t: (b, 0, c)),
            pl.BlockSpec((None, 1, tc), lambda b, c, t: (b, 0, c)),
        ],
        out_shape=[
            jax.ShapeDtypeStruct((B, T, W_A), BF16),
            jax.ShapeDtypeStruct((B, CONV_WIDTH - 1, W_A), F32),
            jax.ShapeDtypeStruct((B, 1, W_A), F32),
        ],
        scratch_shapes=[pltpu.VMEM((SUBLANES + tt, tc), F32), pltpu.VMEM((1, tc), F32)],
        compiler_params=_params(("parallel", "parallel", "arbitrary")),
        name="branch_a",
    )(z, z, conv_buf, h0, conv_w, conv_b, lru_wa, lru_ba, lru_wx, lru_bx, lru_lam)


def _branch_b_kernel(p0, xb_ref, gb_ref, pbuf_ref, pw_ref, ps_ref, yb_ref, pnew_ref, hist_scr):
    g = pl.program_id(1)
    t = pl.program_id(2)
    nt = pl.num_programs(2)
    tt, _ = xb_ref.shape

    @pl.when(t == 0)
    def _():
        hist_scr[...] = pbuf_ref[...]

    xb = xb_ref[...]
    ext = jnp.concatenate([hist_scr[...], xb], axis=0)
    s2 = ext + pltpu.roll(ext, 1, axis=0)
    s4 = s2 + pltpu.roll(s2, 2, axis=0)
    s8 = s4 + pltpu.roll(s4, 4, axis=0)
    s16 = s8 + pltpu.roll(s8, 8, axis=0)
    win = jnp.where(g == 0, s2, jnp.where(g == 1, s4, jnp.where(g == 2, s8, s16)))[POOL_HIST:, :]
    w = jnp.where(g == 0, 2, jnp.where(g == 1, 4, jnp.where(g == 2, 8, 16)))
    pos = p0 + t * tt + lax.broadcasted_iota(jnp.int32, (tt, 1), 0)
    cnt = jnp.minimum(w, pos + 1).astype(F32)
    d = (win / cnt - xb).astype(BF16)
    y = jnp.dot(d, pw_ref[...], preferred_element_type=F32) * ps_ref[...]
    gb = gb_ref[...]
    yb_ref[...] = (y * (gb * _sigmoid(gb))).astype(BF16)

    new_hist = ext[tt:tt + POOL_HIST, :]
    hist_scr[...] = new_hist

    @pl.when(t == nt - 1)
    def _():
        pnew_ref[...] = new_hist


def _branch_b(z, pool_buf16, pool_w, pool_scale, l, p0, tt):
    B, T, _ = z.shape
    gd = POOL_GROUP_DIM
    off = 2 * W_A // gd
    return pl.pallas_call(
        functools.partial(_branch_b_kernel, p0),
        grid=(B, N_POOL_GROUPS, T // tt),
        in_specs=[
            pl.BlockSpec((None, tt, gd), lambda b, g, t: (b, t, off + g)),
            pl.BlockSpec((None, tt, gd), lambda b, g, t: (b, t, off + N_POOL_GROUPS + g)),
            pl.BlockSpec((None, POOL_HIST, gd), lambda b, g, t: (b, 0, g)),
            pl.BlockSpec((None, None, gd, gd), lambda b, g, t: (l, g, 0, 0)),
            pl.BlockSpec((None, 1, gd), lambda b, g, t: (l, 0, g)),
        ],
        out_specs=[
            pl.BlockSpec((None, tt, gd), lambda b, g, t: (b, t, g)),
            pl.BlockSpec((None, POOL_HIST, gd), lambda b, g, t: (b, 0, g)),
        ],
        out_shape=[
            jax.ShapeDtypeStruct((B, T, W_B), BF16),
            jax.ShapeDtypeStruct((B, POOL_HIST, W_B), F32),
        ],
        scratch_shapes=[pltpu.VMEM((POOL_HIST, gd), F32)],
        compiler_params=_params(("parallel", "parallel", "arbitrary")),
        name="branch_b",
    )(z, z, pool_buf16, pool_w, pool_scale)


def _merge_kernel(final, ya_ref, yb_ref, ma_ref, mb_ref, x_ref, mod_ref, wpa_ref, wpb_ref, wo_ref, fg_ref,
                  o_ref):
    bb, tt, _ = x_ref.shape
    rows = bb * tt
    pa = jnp.dot(ya_ref[...].reshape(rows, W_A), wpa_ref[...], preferred_element_type=F32)
    pb = jnp.dot(yb_ref[...].reshape(rows, W_B), wpb_ref[...], preferred_element_type=F32)
    ma = ma_ref[...].reshape(rows, D_MODEL)
    mb = mb_ref[...].reshape(rows, D_MODEL)
    m = _sigmoid(ma) * pa + _sigmoid(mb) * pb
    o = jnp.dot(m.astype(BF16), wo_ref[...], preferred_element_type=F32)
    x = x_ref[...] + mod_ref[:, 2:3, :] * o.reshape(bb, tt, D_MODEL)
    if final:
        x = x * lax.rsqrt(jnp.mean(x * x, axis=-1, keepdims=True) + EPS) * fg_ref[...]
    o_ref[...] = x


def _merge(ya, yb, z, x, mod, w_proj_a, w_proj_b, w_out, final_g, l, bb, tt, final):
    B, T, _ = x.shape
    off = (2 * W_A + 2 * W_B) // D_MODEL
    tile = lambda: pl.BlockSpec((bb, tt, D_MODEL), lambda b, t: (b, t, 0))
    weight = lambda: pl.BlockSpec((None, D_MODEL, D_MODEL), lambda b, t: (l, 0, 0),
                                  pipeline_mode=pl.Buffered(1))
    return pl.pallas_call(
        functools.partial(_merge_kernel, final),
        grid=(B // bb, T // tt),
        in_specs=[
            tile(), tile(),
            pl.BlockSpec((bb, tt, D_MODEL), lambda b, t: (b, t, off)),
            pl.BlockSpec((bb, tt, D_MODEL), lambda b, t: (b, t, off + 1)),
            tile(),
            pl.BlockSpec((bb, 3, D_MODEL), lambda b, t: (b, 0, 0)),
            weight(), weight(), weight(),
            pl.BlockSpec((1, D_MODEL), lambda b, t: (0, 0)),
        ],
        out_specs=tile(),
        out_shape=jax.ShapeDtypeStruct((B, T, D_MODEL), F32),
        compiler_params=_params(("parallel", "parallel")),
        name="merge_out",
    )(ya, yb, z, z, x, mod, w_proj_a, w_proj_b, w_out, final_g.reshape(1, D_MODEL))


def kernel(x_prompt, x_sample, c_prompt, c_sample, state_conv, state_lru, state_pool, norm_g, w_ada, b_ada, w_in, conv_w, conv_b, lru_wa, lru_ba, lru_wx, lru_bx, lru_lam, pool_w, pool_scale, w_proj_a, w_proj_b, w_out, final_g):
    bp = x_prompt.shape[0]
    bs = x_sample.shape[0]

    nb = -(-(bp + bs) // SUBLANES) * SUBLANES
    c_all = jnp.concatenate([c_prompt, c_sample, jnp.zeros((nb - bp - bs, D_MODEL), F32)], axis=0)
    mod = _modulation(c_all, w_ada, b_ada)
    mod = mod.reshape(DEPTH, nb, 3, D_MODEL)

    w_in_h = w_in.astype(BF16)
    wa_h = lru_wa.astype(BF16)
    wx_h = lru_wx.astype(BF16)
    pw_h = pool_w.astype(BF16)
    wpa_h = w_proj_a.astype(BF16)
    wpb_h = w_proj_b.astype(BF16)
    wo_h = w_out.astype(BF16)
    row = lambda v: v.reshape(DEPTH, 1, v.shape[-1])

    streams = [
        dict(x=x_prompt, lo=0, hi=bp, p0=0, tiles=dict(in_bb=1, in_tt=1024, a_tt=128, b_tt=256, m_bb=1, m_tt=256),
             conv=[jnp.zeros((bp, CONV_WIDTH - 1, W_A), F32)] * DEPTH,
             lru=[jnp.zeros((bp, 1, W_A), F32)] * DEPTH,
             pool=[jnp.zeros((bp, POOL_HIST, W_B), F32)] * DEPTH),
        dict(x=x_sample, lo=bp, hi=bp + bs, p0=PAST_LEN,
             tiles=dict(in_bb=16, in_tt=64, a_tt=64, b_tt=64, m_bb=4, m_tt=64),
             conv=[state_conv[l] for l in range(DEPTH)],
             lru=[state_lru[l].reshape(bs, 1, W_A) for l in range(DEPTH)],
             pool=[jnp.pad(state_pool[l], ((0, 0), (1, 0), (0, 0))) for l in range(DEPTH)]),
    ]

    outs = []
    for s in streams:
        x = s["x"]
        tl = s["tiles"]
        conv_new, lru_new, pool_new = [], [], []
        for l in range(DEPTH):
            mod_l = mod[l, s["lo"]:s["hi"]]
            z = _inproj(x, mod_l, norm_g, w_in_h, l, tl["in_bb"], tl["in_tt"])
            ya, cn, hl = _branch_a(z, s["conv"][l], s["lru"][l], conv_w, row(conv_b), wa_h, row(lru_ba), wx_h,
                                   row(lru_bx), row(lru_lam), l, tl["a_tt"])
            yb, pn = _branch_b(z, s["pool"][l], pw_h, row(pool_scale), l, s["p0"], tl["b_tt"])
            x = _merge(ya, yb, z, x, mod_l, wpa_h, wpb_h, wo_h, final_g, l, tl["m_bb"], tl["m_tt"],
                       final=(l == DEPTH - 1))
            conv_new.append(cn)
            lru_new.append(hl.reshape(hl.shape[0], W_A))
            pool_new.append(pn[:, 1:, :])
        outs.append((x, jnp.stack(conv_new), jnp.stack(lru_new), jnp.stack(pool_new)))

    (yp, cp, lp, pp), (ys, cs, ls, ps) = outs
    return (yp, ys, cp, lp, pp, cs, ls, ps)
```

```python
import functools

import jax
import jax.numpy as jnp
from jax import lax
from jax.experimental import pallas as pl
from jax.experimental.pallas import tpu as pltpu

D_MODEL = 2048
DEPTH = 2
W_A = D_MODEL
N_LRU_HEADS = 16
LRU_HEAD_DIM = W_A // N_LRU_HEADS
CONV_WIDTH = 4
LRU_C = 8.0
W_B = D_MODEL
POOL_WINDOWS = (2, 4, 8, 16)
N_POOL_GROUPS = 4
POOL_GROUP_DIM = W_B // N_POOL_GROUPS
POOL_PAD = POOL_WINDOWS[-1] - 1
D_IN = 2 * W_A + 2 * W_B + 2 * D_MODEL
EPS = 1e-6
PAST_LEN = 2048

F32 = jnp.float32
BF16 = jnp.bfloat16
F32_MIN_NORMAL = float(jnp.finfo(jnp.float32).tiny)

V7X_VMEM_LIMIT_BYTES = 56 * 1024 * 1024
SUBLANES = 8
POOL_HIST = POOL_PAD + 1
SLAB = POOL_GROUP_DIM
N_SLABS = W_A // SLAB
HEADS_PER_SLAB = SLAB // LRU_HEAD_DIM
JOB_COLS = 2 * SLAB
N_JOBS = D_IN // JOB_COLS
N_MIX_JOBS = 2 * N_SLABS
ROW_BLOCK = 256


def _params(semantics):
    return pltpu.CompilerParams(dimension_semantics=semantics,
                                vmem_limit_bytes=V7X_VMEM_LIMIT_BYTES)


def _sigmoid(x):
    return jax.nn.sigmoid(x)


def _mod_kernel(c_ref, w_ref, b_ref, o_ref):
    c = c_ref[...]
    s = (c * _sigmoid(c)).astype(BF16)
    o_ref[...] = jnp.dot(s, w_ref[...].astype(BF16), preferred_element_type=F32) + b_ref[...]


def _modulation(c_all, w_ada, b_ada, tn=512):
    nb = c_all.shape[0]
    return pl.pallas_call(
        _mod_kernel,
        grid=(DEPTH, 3 * D_MODEL // tn),
        in_specs=[
            pl.BlockSpec((nb, D_MODEL), lambda l, j: (0, 0)),
            pl.BlockSpec((None, D_MODEL, tn), lambda l, j: (l, 0, j)),
            pl.BlockSpec((None, 1, tn), lambda l, j: (l, 0, j)),
        ],
        out_specs=pl.BlockSpec((None, nb, tn), lambda l, j: (l, 0, j)),
        out_shape=jax.ShapeDtypeStruct((DEPTH, nb, 3 * D_MODEL), F32),
        compiler_params=_params(("arbitrary", "arbitrary")),
        name="adaln_mod",
    )(c_all, w_ada, b_ada.reshape(DEPTH, 1, 3 * D_MODEL))


def _linear_scan(a, u, h_in):
    n, c = a.shape
    g = n // SUBLANES
    a3 = a.reshape(g, SUBLANES, c)
    u3 = u.reshape(g, SUBLANES, c)
    row = lax.broadcasted_iota(jnp.int32, (g, SUBLANES, c), 1)
    d = 1
    while d < SUBLANES:
        keep = row >= d
        a_sh = pltpu.roll(a3, d, axis=1)
        u_sh = pltpu.roll(u3, d, axis=1)
        u3 = u3 + jnp.where(keep, a3 * u_sh, 0.0)
        a3 = a3 * jnp.where(keep, a_sh, 1.0)
        d *= 2
    h = h_in
    out = []
    for k in range(g):
        hk = a3[k] * h + u3[k]
        out.append(hk)
        h = hk[SUBLANES - 1:SUBLANES, :]
    return jnp.concatenate(out, axis=0), h


def _mix_a(xa, xs1, xs2, xs3, ga, h_in, cw, cb, wg_ref, ba, bx, decay):
    xc = cb + xs3 * cw[0:1, :]
    xc = xc + xs2 * cw[1:2, :]
    xc = xc + xs1 * cw[2:3, :]
    xc = xc + xa * cw[3:4, :]
    xh = xc.astype(BF16)
    r_parts, i_parts = [], []
    for hh in range(HEADS_PER_SLAB):
        gts = jnp.dot(xh[:, hh * LRU_HEAD_DIM:(hh + 1) * LRU_HEAD_DIM], wg_ref[hh],
                      preferred_element_type=F32)
        r_parts.append(gts[:, :LRU_HEAD_DIM])
        i_parts.append(gts[:, LRU_HEAD_DIM:])
    r = _sigmoid(jnp.concatenate(r_parts, axis=-1) + ba)
    i = _sigmoid(jnp.concatenate(i_parts, axis=-1) + bx)
    log_a = r * decay
    a = jnp.exp(log_a)
    th = jnp.tanh(log_a)
    neg_expm1 = (-2.0 * th) / (1.0 - th)
    y = jnp.maximum(neg_expm1, 0.0)
    mult = y * lax.rsqrt(jnp.maximum(y, F32_MIN_NORMAL))
    u = mult * i * xc
    hs, h_out = _linear_scan(a, u, h_in)
    return (hs * (ga * _sigmoid(ga))).astype(BF16), h_out


def _mix_b(ext, gb, g, pos0, first_chunk, pw_ref, ps):
    s2 = ext + pltpu.roll(ext, 1, axis=0)
    s4 = s2 + pltpu.roll(s2, 2, axis=0)
    s8 = s4 + pltpu.roll(s4, 4, axis=0)
    s16 = s8 + pltpu.roll(s8, 8, axis=0)
    win = jnp.where(g == 0, s2, jnp.where(g == 1, s4, jnp.where(g == 2, s8, s16)))[POOL_HIST:, :]
    xb = ext[POOL_HIST:, :]
    w, inv_w = POOL_WINDOWS[-1], 1.0 / POOL_WINDOWS[-1]
    for gi in range(N_POOL_GROUPS - 2, -1, -1):
        w = jnp.where(g == gi, POOL_WINDOWS[gi], w)
        inv_w = jnp.where(g == gi, 1.0 / POOL_WINDOWS[gi], inv_w)
    d = win * inv_w - xb
    if first_chunk:
        pos = pos0 + lax.broadcasted_iota(jnp.int32, (POOL_HIST, xb.shape[1]), 0)
        cnt = jnp.minimum(w, pos + 1).astype(F32)
        d_head = win[:POOL_HIST, :] / cnt - xb[:POOL_HIST, :]
        d = jnp.concatenate([d_head, d[POOL_HIST:, :]], axis=0) if d.shape[0] > POOL_HIST else d_head
    y = jnp.dot(d.astype(BF16), pw_ref[...], preferred_element_type=F32) * ps
    return (y * (gb * _sigmoid(gb))).astype(BF16)


def _fused_in_kernel(p0, x_ref, mod_ref, g_ref, wl_ref, wr_ref, cst_ref, hst_ref, pst_ref,
                     cw_ref, cb_ref, wg_ref, ba_ref, bx_ref, lam_ref, pw_ref, ps_ref,
                     ya_ref, yb_ref, mm_ref, cnew_ref, hnew_ref, pnew_ref,
                     h_scr, conv_scr, hcar_scr, pool_scr):
    bb, tt, _ = x_ref.shape
    rows = bb * tt
    rb_rows = min(rows, ROW_BLOCK)
    ch = min(tt, rb_rows)
    seqs_per_rb = rb_rows // ch
    n_rb = rows // rb_rows
    t = pl.program_id(1)
    s = pl.program_id(2)

    def project(rb):
        hv = h_scr[rb * rb_rows:(rb + 1) * rb_rows, :]
        return (jnp.dot(hv, wl_ref[...], preferred_element_type=F32),
                jnp.dot(hv, wr_ref[...], preferred_element_type=F32))

    def chunks(rb):
        for i in range(seqs_per_rb):
            r0 = rb * rb_rows + i * ch
            yield r0 // tt, (r0 % tt) // ch, i * ch

    @pl.when(s == 0)
    def _():
        @pl.when(t == 0)
        def _():
            conv_scr[...] = cst_ref[...]
            hcar_scr[...] = hst_ref[...]
            pool_scr[...] = pst_ref[...]

        x = x_ref[...]
        y = x * lax.rsqrt(jnp.mean(x * x, axis=-1, keepdims=True) + EPS) * g_ref[...]
        h = y * (1.0 + mod_ref[:, 1:2, :]) + mod_ref[:, 0:1, :]
        h_scr[...] = h.reshape(rows, D_MODEL).astype(BF16)

    @pl.when(s < N_SLABS)
    def _():
        j = s
        cw = cw_ref[...]
        cb = cb_ref[...]
        ba = ba_ref[...]
        bx = bx_ref[...]
        decay = (-LRU_C) * jax.nn.softplus(-lam_ref[...])
        row8 = lax.broadcasted_iota(jnp.int32, (SUBLANES, SLAB), 0)
        h_car, prev8 = {}, {}
        z_next = project(0)
        for rb in range(n_rb):
            zl, zr = z_next
            if rb + 1 < n_rb:
                z_next = project(rb + 1)
            for q, c, off in chunks(rb):
                xa = zl[off:off + ch, :]
                ga = zr[off:off + ch, :]
                if c == 0:
                    h_car[q] = hcar_scr[j, q]
                    prev8[q] = conv_scr[j, q]
                first8 = xa[:SUBLANES, :]
                shifted = []
                for k in (1, 2, 3):
                    xs = jnp.where(row8 >= k, pltpu.roll(first8, k, axis=0), pltpu.roll(prev8[q], k, axis=0))
                    if ch > SUBLANES:
                        xs = jnp.concatenate([xs, pltpu.roll(xa, k, axis=0)[SUBLANES:, :]], axis=0)
                    shifted.append(xs)
                ya, h_car[q] = _mix_a(xa, shifted[0], shifted[1], shifted[2], ga, h_car[q], cw, cb, wg_ref,
                                      ba, bx, decay)
                ya_ref[q, c * ch:(c + 1) * ch, :] = ya
                prev8[q] = xa[ch - SUBLANES:, :]
                if (c + 1) * ch == tt:
                    conv_scr[j, q] = prev8[q]
                    hcar_scr[j, q] = h_car[q]
                    cnew_ref[j, q] = prev8[q]
                    hnew_ref[j, q] = h_car[q]

    @pl.when((s >= N_SLABS) & (s < N_MIX_JOBS))
    def _():
        g = s - N_SLABS
        ps = ps_ref[...]
        hist = {}
        z_next = project(0)
        for rb in range(n_rb):
            zl, zr = z_next
            if rb + 1 < n_rb:
                z_next = project(rb + 1)
            for q, c, off in chunks(rb):
                xb = zl[off:off + ch, :]
                gb = zr[off:off + ch, :]
                if c == 0:
                    hist[q] = pool_scr[g, q]
                ext = jnp.concatenate([hist[q], xb], axis=0)
                yb = _mix_b(ext, gb, g, p0 + t * tt, c == 0, pw_ref, ps)
                yb_ref[q, c * ch:(c + 1) * ch, :] = yb
                hist[q] = xb[ch - POOL_HIST:, :]
                if (c + 1) * ch == tt:
                    pool_scr[g, q] = hist[q]
                    pnew_ref[g, q] = hist[q]

    @pl.when(s >= N_MIX_JOBS)
    def _():
        for rb in range(rows // rb_rows):
            zl, zr = project(rb)
            if tt >= rb_rows:
                q, r = (rb * rb_rows) // tt, (rb * rb_rows) % tt
                mm_ref[q, r:r + rb_rows, :SLAB] = zl
                mm_ref[q, r:r + rb_rows, SLAB:] = zr
            else:
                q0 = (rb * rb_rows) // tt
                mm_ref[q0:q0 + seqs_per_rb, :, :SLAB] = zl.reshape(seqs_per_rb, tt, SLAB)
                mm_ref[q0:q0 + seqs_per_rb, :, SLAB:] = zr.reshape(seqs_per_rb, tt, SLAB)


def _fused_in(x, mod, norm_g, w_in, conv_st, lru_st, pool_st, conv_w, conv_b, w_gates, lru_ba, lru_bx, lru_lam,
              pool_w, pool_scale, l, p0, bb, tt):
    B, T, _ = x.shape
    rows = bb * tt
    clip = lambda v, lo, hi: jnp.minimum(jnp.maximum(v, lo), hi)
    ja = lambda s: clip(s, 0, N_SLABS - 1)
    jb = lambda s: clip(s - N_SLABS, 0, N_SLABS - 1)
    jm = lambda s: clip(s - N_MIX_JOBS, 0, N_JOBS - N_MIX_JOBS - 1)

    def left(s):
        return jnp.where(s < N_SLABS, s, jnp.where(s < N_MIX_JOBS, s + N_SLABS, 2 * s))

    def right(s):
        return jnp.where(s < N_SLABS, s + N_SLABS, jnp.where(s < N_MIX_JOBS, s + 2 * N_SLABS, 2 * s + 1))

    vec_a = lambda: pl.BlockSpec((None, 1, SLAB), lambda b, t, s: (l, 0, ja(s)))
    state = lambda r: pl.BlockSpec((N_SLABS, bb, r, SLAB), lambda b, t, s: (0, b, 0, 0))
    return pl.pallas_call(
        functools.partial(_fused_in_kernel, p0),
        grid=(B // bb, T // tt, N_JOBS),
        in_specs=[
            pl.BlockSpec((bb, tt, D_MODEL), lambda b, t, s: (b, t, 0)),
            pl.BlockSpec((bb, 3, D_MODEL), lambda b, t, s: (b, 0, 0)),
            pl.BlockSpec((None, 1, D_MODEL), lambda b, t, s: (l, 0, 0)),
            pl.BlockSpec((None, D_MODEL, SLAB), lambda b, t, s: (l, 0, left(s))),
            pl.BlockSpec((None, D_MODEL, SLAB), lambda b, t, s: (l, 0, right(s))),
            state(SUBLANES), state(1), state(POOL_HIST),
            pl.BlockSpec((None, CONV_WIDTH, SLAB), lambda b, t, s: (l, 0, ja(s))),
            vec_a(),
            pl.BlockSpec((None, HEADS_PER_SLAB, LRU_HEAD_DIM, 2 * LRU_HEAD_DIM), lambda b, t, s: (l, ja(s), 0, 0)),
            vec_a(), vec_a(), vec_a(),
            pl.BlockSpec((None, None, SLAB, SLAB), lambda b, t, s: (l, jb(s), 0, 0)),
            pl.BlockSpec((None, 1, SLAB), lambda b, t, s: (l, 0, jb(s))),
        ],
        out_specs=[
            pl.BlockSpec((bb, tt, SLAB), lambda b, t, s: (b, t, ja(s))),
            pl.BlockSpec((bb, tt, SLAB), lambda b, t, s: (b, t, jb(s))),
            pl.BlockSpec((bb, tt, JOB_COLS), lambda b, t, s: (b, t, jm(s))),
            state(SUBLANES), state(1), state(POOL_HIST),
        ],
        out_shape=[
            jax.ShapeDtypeStruct((B, T, W_A), BF16),
            jax.ShapeDtypeStruct((B, T, W_B), BF16),
            jax.ShapeDtypeStruct((B, T, 2 * D_MODEL), F32),
            jax.ShapeDtypeStruct((N_SLABS, B, SUBLANES, SLAB), F32),
            jax.ShapeDtypeStruct((N_SLABS, B, 1, SLAB), F32),
            jax.ShapeDtypeStruct((N_SLABS, B, POOL_HIST, SLAB), F32),
        ],
        scratch_shapes=[
            pltpu.VMEM((rows, D_MODEL), BF16),
            pltpu.VMEM((N_SLABS, bb, SUBLANES, SLAB), F32),
            pltpu.VMEM((N_SLABS, bb, 1, SLAB), F32),
            pltpu.VMEM((N_SLABS, bb, POOL_HIST, SLAB), F32),
        ],
        compiler_params=_params(("arbitrary", "arbitrary", "arbitrary")),
        name="fused_in",
    )(x, mod, norm_g.reshape(DEPTH, 1, D_MODEL), w_in, w_in, conv_st, lru_st, pool_st,
      conv_w, conv_b, w_gates, lru_ba, lru_bx, lru_lam, pool_w, pool_scale)


def _merge_kernel(final, ya_ref, yb_ref, ma_ref, mb_ref, x_ref, mod_ref, wpa_ref, wpb_ref, wo_ref, fg_ref,
                  o_ref):
    bb, tt, _ = x_ref.shape
    rows = bb * tt
    pa = jnp.dot(ya_ref[...].reshape(rows, W_A), wpa_ref[...], preferred_element_type=F32)
    pb = jnp.dot(yb_ref[...].reshape(rows, W_B), wpb_ref[...], preferred_element_type=F32)
    ma = ma_ref[...].reshape(rows, D_MODEL)
    mb = mb_ref[...].reshape(rows, D_MODEL)
    m = _sigmoid(ma) * pa + _sigmoid(mb) * pb
    o = jnp.dot(m.astype(BF16), wo_ref[...], preferred_element_type=F32)
    x = x_ref[...] + mod_ref[:, 2:3, :] * o.reshape(bb, tt, D_MODEL)
    if final:
        x = x * lax.rsqrt(jnp.mean(x * x, axis=-1, keepdims=True) + EPS) * fg_ref[...]
    o_ref[...] = x


def _merge(ya, yb, mm, x, mod, w_proj_a, w_proj_b, w_out, final_g, l, bb, tt, final):
    B, T, _ = x.shape
    tile = lambda: pl.BlockSpec((bb, tt, D_MODEL), lambda b, t: (b, t, 0))
    weight = lambda: pl.BlockSpec((None, D_MODEL, D_MODEL), lambda b, t: (l, 0, 0),
                                  pipeline_mode=pl.Buffered(1))
    return pl.pallas_call(
        functools.partial(_merge_kernel, final),
        grid=(B // bb, T // tt),
        in_specs=[
            tile(), tile(),
            pl.BlockSpec((bb, tt, D_MODEL), lambda b, t: (b, t, 0)),
            pl.BlockSpec((bb, tt, D_MODEL), lambda b, t: (b, t, 1)),
            tile(),
            pl.BlockSpec((bb, 3, D_MODEL), lambda b, t: (b, 0, 0)),
            weight(), weight(), weight(),
            pl.BlockSpec((1, D_MODEL), lambda b, t: (0, 0)),
        ],
        out_specs=tile(),
        out_shape=jax.ShapeDtypeStruct((B, T, D_MODEL), F32),
        compiler_params=_params(("arbitrary", "arbitrary")),
        name="merge_out",
    )(ya, yb, mm, mm, x, mod, w_proj_a, w_proj_b, w_out, final_g.reshape(1, D_MODEL))


def _to_slabs(v, pad_front):
    b, r, _ = v.shape
    v = v.reshape(b, r, N_SLABS, SLAB).transpose(2, 0, 1, 3)
    return jnp.pad(v, ((0, 0), (0, 0), (pad_front, 0), (0, 0)))


def _from_slabs(v, skip_front):
    v = v[:, :, skip_front:, :].transpose(1, 2, 0, 3)
    return v.reshape(v.shape[0], v.shape[1], N_SLABS * SLAB)


def kernel(x_prompt, x_sample, c_prompt, c_sample, state_conv, state_lru, state_pool, norm_g, w_ada, b_ada, w_in, conv_w, conv_b, lru_wa, lru_ba, lru_wx, lru_bx, lru_lam, pool_w, pool_scale, w_proj_a, w_proj_b, w_out, final_g):
    bp = x_prompt.shape[0]
    bs = x_sample.shape[0]

    nb = -(-(bp + bs) // SUBLANES) * SUBLANES
    c_all = jnp.concatenate([c_prompt, c_sample, jnp.zeros((nb - bp - bs, D_MODEL), F32)], axis=0)
    mod = _modulation(c_all, w_ada, b_ada)
    mod = mod.reshape(DEPTH, nb, 3, D_MODEL)

    w_in_h = w_in.astype(BF16)
    w_gates = jnp.concatenate([lru_wa, lru_wx], axis=-1).astype(BF16)
    pw_h = pool_w.astype(BF16)
    wpa_h = w_proj_a.astype(BF16)
    wpb_h = w_proj_b.astype(BF16)
    wo_h = w_out.astype(BF16)
    row = lambda v: v.reshape(DEPTH, 1, v.shape[-1])

    conv_keep = SUBLANES - (CONV_WIDTH - 1)
    streams = [
        dict(x=x_prompt, lo=0, hi=bp, p0=0, in_bb=1, in_tt=1024, m_bb=1, m_tt=256,
             conv=[jnp.zeros((bp, CONV_WIDTH - 1, W_A), F32)] * DEPTH,
             lru=[jnp.zeros((bp, 1, W_A), F32)] * DEPTH,
             pool=[jnp.zeros((bp, POOL_PAD, W_B), F32)] * DEPTH),
        dict(x=x_sample, lo=bp, hi=bp + bs, p0=PAST_LEN, in_bb=8, in_tt=64, m_bb=4, m_tt=64,
             conv=[state_conv[l] for l in range(DEPTH)],
             lru=[state_lru[l].reshape(bs, 1, W_A) for l in range(DEPTH)],
             pool=[state_pool[l] for l in range(DEPTH)]),
    ]

    outs = []
    for st in streams:
        x = st["x"]
        conv_new, lru_new, pool_new = [], [], []
        for l in range(DEPTH):
            mod_l = mod[l, st["lo"]:st["hi"]]
            ya, yb, mm, cn, hn, pn = _fused_in(
                x, mod_l, norm_g, w_in_h,
                _to_slabs(st["conv"][l], conv_keep), _to_slabs(st["lru"][l], 0), _to_slabs(st["pool"][l], 1),
                conv_w, row(conv_b), w_gates, row(lru_ba), row(lru_bx), row(lru_lam), pw_h, row(pool_scale),
                l, st["p0"], st["in_bb"], st["in_tt"])
            x = _merge(ya, yb, mm, x, mod_l, wpa_h, wpb_h, wo_h, final_g, l, st["m_bb"], st["m_tt"],
                       final=(l == DEPTH - 1))
            conv_new.append(_from_slabs(cn, conv_keep))
            lru_new.append(_from_slabs(hn, 0)[:, 0, :])
            pool_new.append(_from_slabs(pn, 1))
        outs.append((x, jnp.stack(conv_new), jnp.stack(lru_new), jnp.stack(pool_new)))

    (yp, cp, lp, pp), (ys, cs, ls, ps) = outs
    return (yp, ys, cp, lp, pp, cs, ls, ps)
```
